```python
import jax, jax.numpy as jnp
from jax import lax
import numpy as np

D_MODEL = 2048
BATCH = 2
SEQ = 8192
DEPTH = 2

GRID_W = 64
Q_BLOCK = 128
NORM_EPS = 1e-6
ROPE_THETA = 500000.0
AXIAL_THETA = 10000.0
MLA_HEADS = 8
MLA_Q_LORA = 512
MLA_KV_LORA = 256
MLA_NOPE_DIM = 128
MLA_ROPE_DIM = 64
MLA_QK_DIM = MLA_NOPE_DIM + MLA_ROPE_DIM
MLA_V_DIM = 128
GQA_HEADS = 8
GQA_KV_HEADS = 2
GQA_HEAD_DIM = 128
SWA_HEADS = 32
SWA_KV_HEADS = 4
SWA_HEAD_DIM = 64
SWA_WINDOW = 128
SWA_ROT_DIM = SWA_HEAD_DIM // 4
D_FF = 4 * D_MODEL
EVEN_IN = MLA_Q_LORA + MLA_KV_LORA + MLA_ROPE_DIM + (GQA_HEADS + 2 * GQA_KV_HEADS) * GQA_HEAD_DIM
EVEN_OUT = MLA_HEADS * MLA_V_DIM + GQA_HEADS * GQA_HEAD_DIM
ODD_IN = (SWA_HEADS + 2 * SWA_KV_HEADS) * SWA_HEAD_DIM
ODD_OUT = SWA_HEADS * SWA_HEAD_DIM
N_EVEN = (DEPTH + 1) // 2
N_ODD = DEPTH // 2

kernel_name = 'hybrid_mla_gridgqa_swa_sqrelu_encoder'


def rms_norm(x, gain):
    xf = x.astype(jnp.float32)
    y = xf * lax.rsqrt(jnp.mean(xf * xf, axis=-1, keepdims=True) + NORM_EPS)
    return (y * gain.astype(jnp.float32)).astype(x.dtype)


def rope_table(pos, dim, theta):
    inv = jnp.float32(theta) ** (-jnp.arange(0, dim, 2, dtype=jnp.float32) / dim)
    ang = pos.astype(jnp.float32)[:, None] * inv[None, :]
    return jnp.cos(ang), jnp.sin(ang)


def apply_rope(x, cos, sin):
    half = x.shape[-1] // 2
    c = cos[None, :, None, :].astype(x.dtype)
    s = sin[None, :, None, :].astype(x.dtype)
    x1 = x[..., :half]
    x2 = x[..., half:]
    return jnp.concatenate([x1 * c - x2 * s, x2 * c + x1 * s], axis=-1)


def dense_block_attention(q, k, v, scale):
    B, S, Hq, dk = q.shape
    Hkv = k.shape[2]
    G = Hq // Hkv
    dv = v.shape[-1]
    nb = S // Q_BLOCK
    qb = jnp.swapaxes(q.reshape(B, nb, Q_BLOCK, Hkv, G, dk), 0, 1)

    def one_block(qblk):
        s = jnp.einsum('bqhgd,bkhd->bhgqk', qblk, k, preferred_element_type=jnp.float32) * scale
        p = jax.nn.softmax(s, axis=-1).astype(v.dtype)
        return jnp.einsum('bhgqk,bkhd->bqhgd', p, v)

    o = lax.map(one_block, qb)
    return jnp.swapaxes(o, 0, 1).reshape(B, S, Hq, dv)


def banded_window_attention(q, k, v, sink, scale):
    B, S, Hq, d = q.shape
    Hkv = k.shape[2]
    G = Hq // Hkv
    nb = S // Q_BLOCK
    span = Q_BLOCK + 2 * SWA_WINDOW
    pad = ((0, 0), (SWA_WINDOW, SWA_WINDOW), (0, 0), (0, 0))
    kp = jnp.pad(k, pad)
    vp = jnp.pad(v, pad)
    qb = jnp.swapaxes(q.reshape(B, nb, Q_BLOCK, Hkv, G, d), 0, 1)
    sink_b = sink.astype(jnp.float32).reshape(1, Hkv, G, 1, 1)
    offs_q = jnp.arange(Q_BLOCK)
    offs_k = jnp.arange(span) - SWA_WINDOW

    def one_block(args):
        i, qblk = args
        start = i * Q_BLOCK
        kblk = lax.dynamic_slice_in_dim(kp, start, span, axis=1)
        vblk = lax.dynamic_slice_in_dim(vp, start, span, axis=1)
        q_pos = start + offs_q
        k_pos = start + offs_k
        valid = (jnp.abs(q_pos[:, None] - k_pos[None, :]) <= SWA_WINDOW) & ((k_pos >= 0) & (k_pos < S))[None, :]
        s = jnp.einsum('bqhgd,bkhd->bhgqk', qblk, kblk, preferred_element_type=jnp.float32) * scale
        s = jnp.where(valid, s, -jnp.inf)
        m = jnp.maximum(jnp.max(s, axis=-1, keepdims=True), sink_b)
        p = jnp.exp(s - m)
        denom = jnp.sum(p, axis=-1, keepdims=True) + jnp.exp(sink_b - m)
        p = (p / denom).astype(v.dtype)
        return jnp.einsum('bhgqk,bkhd->bqhgd', p, vblk)

    o = lax.map(one_block, (jnp.arange(nb), qb))
    return jnp.swapaxes(o, 0, 1).reshape(B, S, Hq, d)


def even_mixer(h, w_in, q_lat_norm, kv_lat_norm, w_uq, w_ukv, q_norm, k_nope_norm, k_rope_norm,
               g_q_norm, g_k_norm, w_out, mla_cos, mla_sin, row_cos, row_sin, col_cos, col_sin):
    B, S, _ = h.shape
    proj = h @ w_in
    o1 = MLA_Q_LORA
    o2 = o1 + MLA_KV_LORA
    o3 = o2 + MLA_ROPE_DIM
    o4 = o3 + GQA_HEADS * GQA_HEAD_DIM
    o5 = o4 + GQA_KV_HEADS * GQA_HEAD_DIM
    c_q = proj[..., :o1]
    c_kv = proj[..., o1:o2]
    k_rope = proj[..., o2:o3]
    q_g = proj[..., o3:o4].reshape(B, S, GQA_HEADS, GQA_HEAD_DIM)
    k_g = proj[..., o4:o5].reshape(B, S, GQA_KV_HEADS, GQA_HEAD_DIM)
    v_g = proj[..., o5:].reshape(B, S, GQA_KV_HEADS, GQA_HEAD_DIM)

    q_a = (rms_norm(c_q, q_lat_norm) @ w_uq).reshape(B, S, MLA_HEADS, MLA_QK_DIM)
    q_a = rms_norm(q_a, q_norm)
    q_a = jnp.concatenate([q_a[..., :MLA_NOPE_DIM], apply_rope(q_a[..., MLA_NOPE_DIM:], mla_cos, mla_sin)], axis=-1)
    kv = (rms_norm(c_kv, kv_lat_norm) @ w_ukv).reshape(B, S, MLA_HEADS, MLA_NOPE_DIM + MLA_V_DIM)
    k_nope = rms_norm(kv[..., :MLA_NOPE_DIM], k_nope_norm)
    v_a = kv[..., MLA_NOPE_DIM:]
    k_r = apply_rope(rms_norm(k_rope, k_rope_norm)[:, :, None, :], mla_cos, mla_sin)
    k_a = jnp.concatenate([k_nope, jnp.broadcast_to(k_r, (B, S, MLA_HEADS, MLA_ROPE_DIM))], axis=-1)
    o_a = dense_block_attention(q_a, k_a, v_a, MLA_QK_DIM ** -0.5)

    half = GQA_HEAD_DIM // 2
    def axial(t):
        return jnp.concatenate([apply_rope(t[..., :half], row_cos, row_sin), apply_rope(t[..., half:], col_cos, col_sin)], axis=-1)
    q_g = axial(rms_norm(q_g, g_q_norm))
    k_g = axial(rms_norm(k_g, g_k_norm))
    o_g = dense_block_attention(q_g, k_g, v_g, GQA_HEAD_DIM ** -0.5)

    merged = jnp.concatenate([o_a.reshape(B, S, -1), o_g.reshape(B, S, -1)], axis=-1)
    return merged @ w_out


def odd_mixer(h, w_qkv, q_norm, k_norm, sink, w_out, swa_cos, swa_sin):
    B, S, _ = h.shape
    qkv = h @ w_qkv
    nq = SWA_HEADS * SWA_HEAD_DIM
    nk = SWA_KV_HEADS * SWA_HEAD_DIM
    q = rms_norm(qkv[..., :nq].reshape(B, S, SWA_HEADS, SWA_HEAD_DIM), q_norm)
    k = rms_norm(qkv[..., nq:nq + nk].reshape(B, S, SWA_KV_HEADS, SWA_HEAD_DIM), k_norm)
    v = qkv[..., nq + nk:].reshape(B, S, SWA_KV_HEADS, SWA_HEAD_DIM)
    q = jnp.concatenate([apply_rope(q[..., :SWA_ROT_DIM], swa_cos, swa_sin), q[..., SWA_ROT_DIM:]], axis=-1)
    k = jnp.concatenate([apply_rope(k[..., :SWA_ROT_DIM], swa_cos, swa_sin), k[..., SWA_ROT_DIM:]], axis=-1)
    o = banded_window_attention(q, k, v, sink, SWA_HEAD_DIM ** -0.5)
    return o.reshape(B, S, -1) @ w_out


def squared_relu_mlp(x, gain, w_up, w_down):
    h = rms_norm(x, gain) @ w_up
    return jnp.square(jax.nn.relu(h)) @ w_down


def _dense(k, shape):
    return jax.random.normal(k, shape, jnp.float32) * (shape[-2] ** -0.5)


def _gain(k, shape):
    return 1.0 + 0.02 * jax.random.normal(k, shape, jnp.float32)


def setup_inputs(seed: int = 0) -> dict:
    key = jax.random.key(seed)
    ks = jax.random.split(key, 22)
    return {
        'x': jax.random.normal(ks[0], (BATCH, SEQ, D_MODEL), jnp.float32),
        'even_norm': _gain(ks[1], (N_EVEN, D_MODEL)),
        'even_w_in': _dense(ks[2], (N_EVEN, D_MODEL, EVEN_IN)),
        'mla_q_lat_norm': _gain(ks[3], (N_EVEN, MLA_Q_LORA)),
        'mla_kv_lat_norm': _gain(ks[4], (N_EVEN, MLA_KV_LORA)),
        'mla_w_uq': _dense(ks[5], (N_EVEN, MLA_Q_LORA, MLA_HEADS * MLA_QK_DIM)),
        'mla_w_ukv': _dense(ks[6], (N_EVEN, MLA_KV_LORA, MLA_HEADS * (MLA_NOPE_DIM + MLA_V_DIM))),
        'mla_q_norm': _gain(ks[7], (N_EVEN, MLA_QK_DIM)),
        'mla_k_nope_norm': _gain(ks[8], (N_EVEN, MLA_NOPE_DIM)),
        'mla_k_rope_norm': _gain(ks[9], (N_EVEN, MLA_ROPE_DIM)),
        'gqa_q_norm': _gain(ks[10], (N_EVEN, GQA_HEAD_DIM)),
        'gqa_k_norm': _gain(ks[11], (N_EVEN, GQA_HEAD_DIM)),
        'even_w_out': _dense(ks[12], (N_EVEN, EVEN_OUT, D_MODEL)),
        'odd_norm': _gain(ks[13], (N_ODD, D_MODEL)),
        'odd_w_qkv': _dense(ks[14], (N_ODD, D_MODEL, ODD_IN)),
        'swa_q_norm': _gain(ks[15], (N_ODD, SWA_HEAD_DIM)),
        'swa_k_norm': _gain(ks[16], (N_ODD, SWA_HEAD_DIM)),
        'swa_sink': jax.random.normal(ks[17], (N_ODD, SWA_HEADS), jnp.float32),
        'odd_w_out': _dense(ks[18], (N_ODD, ODD_OUT, D_MODEL)),
        'mlp_norm': _gain(ks[19], (DEPTH, D_MODEL)),
        'mlp_w_up': _dense(ks[20], (DEPTH, D_MODEL, D_FF)),
        'mlp_w_down': _dense(ks[21], (DEPTH, D_FF, D_MODEL)),
    }


def reference(x, even_norm, even_w_in, mla_q_lat_norm, mla_kv_lat_norm, mla_w_uq, mla_w_ukv,
              mla_q_norm, mla_k_nope_norm, mla_k_rope_norm, gqa_q_norm, gqa_k_norm, even_w_out,
              odd_norm, odd_w_qkv, swa_q_norm, swa_k_norm, swa_sink, odd_w_out,
              mlp_norm, mlp_w_up, mlp_w_down):
    B, S, _ = x.shape
    rows = S // GRID_W
    pos = jnp.arange(S)
    row_pos = jnp.repeat(jnp.arange(rows), GRID_W)
    col_pos = jnp.tile(jnp.arange(GRID_W), rows)
    mla_cos, mla_sin = rope_table(pos, MLA_ROPE_DIM, ROPE_THETA)
    row_cos, row_sin = rope_table(row_pos, GQA_HEAD_DIM // 2, AXIAL_THETA)
    col_cos, col_sin = rope_table(col_pos, GQA_HEAD_DIM // 2, AXIAL_THETA)
    swa_cos, swa_sin = rope_table(pos, SWA_ROT_DIM, ROPE_THETA)
    for layer in range(DEPTH):
        i = layer // 2
        if layer % 2 == 0:
            x = x + even_mixer(rms_norm(x, even_norm[i]), even_w_in[i], mla_q_lat_norm[i], mla_kv_lat_norm[i],
                               mla_w_uq[i], mla_w_ukv[i], mla_q_norm[i], mla_k_nope_norm[i], mla_k_rope_norm[i],
                               gqa_q_norm[i], gqa_k_norm[i], even_w_out[i],
                               mla_cos, mla_sin, row_cos, row_sin, col_cos, col_sin)
        else:
            x = x + odd_mixer(rms_norm(x, odd_norm[i]), odd_w_qkv[i], swa_q_norm[i], swa_k_norm[i],
                              swa_sink[i], odd_w_out[i], swa_cos, swa_sin)
        x = x + squared_relu_mlp(x, mlp_norm[layer], mlp_w_up[layer], mlp_w_down[layer])
    return x
```

```python
import functools
import math

import jax
import jax.numpy as jnp
from jax import lax
from jax.experimental import pallas as pl
from jax.experimental.pallas import tpu as pltpu

D_MODEL = 2048
GRID_W = 64
NORM_EPS = 1e-6
ROPE_THETA = 500000.0
AXIAL_THETA = 10000.0
MLA_HEADS = 8
MLA_Q_LORA = 512
MLA_KV_LORA = 256
MLA_NOPE = 128
MLA_ROPE = 64
MLA_QK = MLA_NOPE + MLA_ROPE
MLA_V = 128
GQA_HEADS = 8
GQA_KV = 2
GQA_D = 128
SWA_HEADS = 32
SWA_KV = 4
SWA_D = 64
SWA_WINDOW = 128
SWA_ROT = SWA_D // 4
D_FF = 4 * D_MODEL

LANES = 128
LOG2E = math.log2(math.e)
VMEM_LIMIT = 56 * 1024 * 1024

F32 = jnp.float32
BF16 = jnp.bfloat16


def _cparams(sem):
    return pltpu.CompilerParams(dimension_semantics=sem, vmem_limit_bytes=VMEM_LIMIT)


def _dot(a, b):
    return jnp.dot(a, b, preferred_element_type=F32)


def _rms(t, gain, n):
    ss = jnp.sum(t * t, axis=-1, keepdims=True)
    return t * lax.rsqrt(ss * (1.0 / n) + NORM_EPS) * gain


def _rot_partner(t, half, group):
    lane = lax.broadcasted_iota(jnp.int32, t.shape, 1)
    first = (lane & (group - 1)) < half
    return jnp.where(first, pltpu.roll(t, LANES - half, 1), pltpu.roll(t, half, 1))


def _rope(t, cos, sin_signed, half, group):
    return t * cos + _rot_partner(t, half, group) * sin_signed


EVEN_W = MLA_Q_LORA + MLA_KV_LORA + LANES + (GQA_HEADS + 2 * GQA_KV) * GQA_D
O_CKV = MLA_Q_LORA
O_KR = O_CKV + MLA_KV_LORA
O_QG = O_KR + LANES
O_KG = O_QG + GQA_HEADS * GQA_D
O_VG = O_KG + GQA_KV * GQA_D


def _even_prep_kernel(x_ref, nrm_ref, win_ref, qlat_ref, kvlat_ref, wuq_ref, wukv_ref,
                      qn_nope_ref, qn_rope_ref, kn_nope_ref, kn_rope_ref, gqn_ref, gkn_ref,
                      mc_ref, ms_ref, ac_ref, as_ref,
                      mq_ref, mk_ref, mv_ref, gq_ref, gk_ref, gv_ref):
    x = x_ref[...]
    h = _rms(x, nrm_ref[...], D_MODEL).astype(BF16)
    proj = _dot(h, win_ref[...])

    mc, ms = mc_ref[...], ms_ref[...]
    ac, asn = ac_ref[...], as_ref[...]
    mla_scale = MLA_QK ** -0.5 * LOG2E
    gqa_scale = GQA_D ** -0.5 * LOG2E

    cq = _rms(proj[:, :O_CKV], qlat_ref[...], MLA_Q_LORA).astype(BF16)
    qa = _dot(cq, wuq_ref[...])
    for hd in range(MLA_HEADS):
        nope = qa[:, hd * LANES:(hd + 1) * LANES]
        rp = qa[:, (MLA_HEADS + hd) * LANES:(MLA_HEADS + hd + 1) * LANES]
        ss = jnp.sum(nope * nope, axis=-1, keepdims=True) + jnp.sum(rp * rp, axis=-1, keepdims=True)
        inv = lax.rsqrt(ss * (1.0 / MLA_QK) + NORM_EPS)
        nope_n = nope * inv * qn_nope_ref[...]
        rp_n = _rope(rp * inv * qn_rope_ref[...], mc, ms, MLA_ROPE // 2, MLA_ROPE)
        mq_ref[hd, :, :LANES] = (nope_n * mla_scale).astype(BF16)
        mq_ref[hd, :, LANES:] = (rp_n * mla_scale).astype(BF16)

    ckv = _rms(proj[:, O_CKV:O_KR], kvlat_ref[...], MLA_KV_LORA).astype(BF16)
    kv = _dot(ckv, wukv_ref[...])
    k_r = _rope(_rms(proj[:, O_KR:O_QG], kn_rope_ref[...], MLA_ROPE), mc, ms, MLA_ROPE // 2, MLA_ROPE).astype(BF16)
    for hd in range(MLA_HEADS):
        kn = kv[:, 2 * hd * LANES:(2 * hd + 1) * LANES]
        mk_ref[hd, :, :LANES] = _rms(kn, kn_nope_ref[...], MLA_NOPE).astype(BF16)
        mk_ref[hd, :, LANES:] = k_r
        mv_ref[hd] = kv[:, (2 * hd + 1) * LANES:(2 * hd + 2) * LANES].astype(BF16)

    for hd in range(GQA_HEADS):
        t = _rms(proj[:, O_QG + hd * GQA_D:O_QG + (hd + 1) * GQA_D], gqn_ref[...], GQA_D)
        gq_ref[hd] = (_rope(t, ac, asn, GQA_D // 4, GQA_D // 2) * gqa_scale).astype(BF16)
    for hd in range(GQA_KV):
        t = _rms(proj[:, O_KG + hd * GQA_D:O_KG + (hd + 1) * GQA_D], gkn_ref[...], GQA_D)
        gk_ref[hd] = _rope(t, ac, asn, GQA_D // 4, GQA_D // 2).astype(BF16)
        gv_ref[hd] = proj[:, O_VG + hd * GQA_D:O_VG + (hd + 1) * GQA_D].astype(BF16)


def _full(shape):
    nd = len(shape)
    return pl.BlockSpec(shape, lambda *_: (0,) * nd)


def _even_prep(x, nrm, win, qlat, kvlat, wuq, wukv, qn_nope, qn_rope, kn_nope, kn_rope, gqn, gkn,
               mc, ms, ac, asn, tm=256):
    B, S, D = x.shape
    nt = S // tm
    row = lambda w: pl.BlockSpec((None, tm, w), lambda b, i: (b, i, 0))
    tab = pl.BlockSpec((tm, LANES), lambda b, i: (i, 0))
    heads = lambda nh, w: pl.BlockSpec((None, nh, tm, w), lambda b, i: (b, 0, i, 0))
    ins = [x, nrm, win, qlat, kvlat, wuq, wukv, qn_nope, qn_rope, kn_nope, kn_rope, gqn, gkn, mc, ms, ac, asn]
    in_specs = [row(D)] + [_full(a.shape) for a in ins[1:13]] + [tab] * 4
    out_shape = [
        jax.ShapeDtypeStruct((B, MLA_HEADS, S, 2 * LANES), BF16),
        jax.ShapeDtypeStruct((B, MLA_HEADS, S, 2 * LANES), BF16),
        jax.ShapeDtypeStruct((B, MLA_HEADS, S, MLA_V), BF16),
        jax.ShapeDtypeStruct((B, GQA_HEADS, S, GQA_D), BF16),
        jax.ShapeDtypeStruct((B, GQA_KV, S, GQA_D), BF16),
        jax.ShapeDtypeStruct((B, GQA_KV, S, GQA_D), BF16),
    ]
    out_specs = [heads(MLA_HEADS, 2 * LANES), heads(MLA_HEADS, 2 * LANES), heads(MLA_HEADS, MLA_V),
                 heads(GQA_HEADS, GQA_D), heads(GQA_KV, GQA_D), heads(GQA_KV, GQA_D)]
    return pl.pallas_call(
        _even_prep_kernel, grid=(B, nt), in_specs=in_specs, out_specs=out_specs, out_shape=out_shape,
        compiler_params=_cparams(("parallel", "parallel")), name="even_prep")(*ins)


def _flash_kernel(q_ref, k_ref, v_ref, o_ref, m_scr, l_scr, acc_scr, *, tk, n_sub):
    G, tq, dk = q_ref.shape
    S = k_ref.shape[0]
    dv = v_ref.shape[1]
    M = G * tq
    sub = M // n_sub
    m_scr[...] = jnp.full(m_scr.shape, -jnp.inf, F32)
    l_scr[...] = jnp.zeros(l_scr.shape, F32)
    acc_scr[...] = jnp.zeros(acc_scr.shape, F32)

    def body(j, carry):
        off = pl.multiple_of(j * tk, tk)
        k = k_ref[pl.ds(off, tk), :]
        v = v_ref[pl.ds(off, tk), :]
        for u in range(n_sub):
            rows = pl.ds(u * sub, sub)
            if sub >= tq:
                q = q_ref[u * sub // tq:(u + 1) * sub // tq].reshape(sub, dk)
            else:
                q = q_ref[u * sub // tq, pl.ds(u * sub % tq, sub), :]
            s = lax.dot_general(q, k, (((1,), (1,)), ((), ())), preferred_element_type=F32)
            m_prev = m_scr[rows, :]
            m_new = jnp.maximum(m_prev, jnp.max(s, axis=-1, keepdims=True))
            alpha = jnp.exp2(m_prev - m_new)
            p = jnp.exp2(s - m_new)
            l_scr[rows, :] = alpha * l_scr[rows, :] + jnp.sum(p, axis=-1, keepdims=True)
            acc_scr[rows, :] = alpha * acc_scr[rows, :] + _dot(p.astype(BF16), v)
            m_scr[rows, :] = m_new
        return carry

    lax.fori_loop(0, S // tk, body, 0)
    out = acc_scr[...] / l_scr[...]
    for g in range(G):
        o_ref[:, g * dv:(g + 1) * dv] = out[g * tq:(g + 1) * tq].astype(o_ref.dtype)


def _flash(q, k, v, tq, tk=512, n_sub=2):
    B, Hkv, G, S, dk = q.shape
    dv = v.shape[-1]
    M = G * tq
    return pl.pallas_call(
        functools.partial(_flash_kernel, tk=tk, n_sub=n_sub),
        grid=(B, Hkv, S // tq),
        in_specs=[pl.BlockSpec((None, None, G, tq, dk), lambda b, h, i: (b, h, 0, i, 0)),
                  pl.BlockSpec((None, None, S, dk), lambda b, h, i: (b, h, 0, 0)),
                  pl.BlockSpec((None, None, S, dv), lambda b, h, i: (b, h, 0, 0))],
        out_specs=pl.BlockSpec((None, tq, G * dv), lambda b, h, i: (b, i, h)),
        out_shape=jax.ShapeDtypeStruct((B, S, Hkv * G * dv), BF16),
        scratch_shapes=[pltpu.VMEM((M, 1), F32), pltpu.VMEM((M, 1), F32), pltpu.VMEM((M, dv), F32)],
        compiler_params=_cparams(("parallel", "parallel", "arbitrary")), name="flash")(q, k, v)


def _proj_res_kernel(*refs):
    x_ref, o_ref = refs[0], refs[-1]
    acc = x_ref[...]
    for a_ref, w_ref in zip(refs[1:-1:2], refs[2:-1:2]):
        acc = acc + _dot(a_ref[...], w_ref[...])
    o_ref[...] = acc


def _proj_res(x, pairs, tm=512):
    T, D = x.shape
    ins, in_specs = [x], [pl.BlockSpec((tm, D), lambda i: (i, 0))]
    for a, w in pairs:
        ins += [a, w]
        in_specs += [pl.BlockSpec((tm, a.shape[1]), lambda i: (i, 0)), _full(w.shape)]
    return pl.pallas_call(
        _proj_res_kernel, grid=(T // tm,), in_specs=in_specs,
        out_specs=pl.BlockSpec((tm, D), lambda i: (i, 0)),
        out_shape=jax.ShapeDtypeStruct((T, D), F32),
        compiler_params=_cparams(("parallel",)), name="proj_res")(*ins)


def _mlp_kernel(x_ref, g_ref, wu_ref, wd_ref, o_ref, xn_scr):
    j = pl.program_id(1)

    @pl.when(j == 0)
    def _():
        x = x_ref[...]
        xn_scr[...] = _rms(x, g_ref[...], D_MODEL).astype(BF16)
        o_ref[...] = x

    h = _dot(xn_scr[...], wu_ref[...])
    h = jnp.square(jnp.maximum(h, 0.0)).astype(BF16)
    o_ref[...] += _dot(h, wd_ref[...])


def _mlp(x, gain, wu, wd, tm=512, fc=1024):
    T, D = x.shape
    F = wu.shape[1]
    return pl.pallas_call(
        _mlp_kernel, grid=(T // tm, F // fc),
        in_specs=[pl.BlockSpec((tm, D), lambda i, j: (i, 0)), _full(gain.shape),
                  pl.BlockSpec((D, fc), lambda i, j: (0, j)), pl.BlockSpec((fc, D), lambda i, j: (j, 0))],
        out_specs=pl.BlockSpec((tm, D), lambda i, j: (i, 0)),
        out_shape=jax.ShapeDtypeStruct((T, D), F32),
        scratch_shapes=[pltpu.VMEM((tm, D), BF16)],
        compiler_params=_cparams(("parallel", "arbitrary")), name="mlp")(x, gain, wu, wd)


ODD_W = SWA_HEADS * SWA_D + 2 * SWA_KV * LANES


def _odd_prep_kernel(x_ref, nrm_ref, w_ref, ones_ref, qn_ref, kn_ref, c_ref, s_ref, q_ref, k_ref, v_ref):
    x = x_ref[...]
    h = _rms(x, nrm_ref[...], D_MODEL).astype(BF16)
    qkv = _dot(h, w_ref[...])
    c, sn = c_ref[...], s_ref[...]
    scale = SWA_D ** -0.5 * LOG2E
    lane = lax.broadcasted_iota(jnp.int32, (x.shape[0], LANES), 1)
    low = lane < SWA_D
    nq = SWA_HEADS * SWA_D

    def norm_rope(t, gain):
        sq = t * t
        hi = sq.astype(BF16)
        lo = (sq - hi.astype(F32)).astype(BF16)
        ss = _dot(hi, ones_ref[...]) + _dot(lo, ones_ref[...])
        tn = t * lax.rsqrt(ss * (1.0 / SWA_D) + NORM_EPS) * gain
        return _rope(tn, c, sn, SWA_ROT // 2, SWA_D)

    for cidx in range(nq // LANES):
        t = norm_rope(qkv[:, cidx * LANES:(cidx + 1) * LANES], qn_ref[...]) * scale
        q_ref[2 * cidx] = jnp.where(low, t, 0.0).astype(BF16)
        q_ref[2 * cidx + 1] = jnp.where(low, 0.0, t).astype(BF16)
    for hd in range(SWA_KV):
        k_ref[hd] = norm_rope(qkv[:, nq + hd * LANES:nq + (hd + 1) * LANES], kn_ref[...]).astype(BF16)
        v_ref[hd] = qkv[:, nq + (SWA_KV + hd) * LANES:nq + (SWA_KV + hd + 1) * LANES].astype(BF16)


def _odd_prep(x, nrm, w, ones_bd, qn, kn, c, sn, tm=256):
    B, S, D = x.shape
    heads = lambda nh: pl.BlockSpec((None, nh, tm, LANES), lambda b, i: (b, 0, i, 0))
    ins = [x, nrm, w, ones_bd, qn, kn, c, sn]
    in_specs = ([pl.BlockSpec((None, tm, D), lambda b, i: (b, i, 0))] + [_full(a.shape) for a in ins[1:6]]
                + [pl.BlockSpec((tm, LANES), lambda b, i: (i, 0))] * 2)
    out_shape = [jax.ShapeDtypeStruct((B, SWA_HEADS, S, LANES), BF16),
                 jax.ShapeDtypeStruct((B, SWA_KV, S, LANES), BF16),
                 jax.ShapeDtypeStruct((B, SWA_KV, S, LANES), BF16)]
    return pl.pallas_call(
        _odd_prep_kernel, grid=(B, S // tm), in_specs=in_specs,
        out_specs=[heads(SWA_HEADS), heads(SWA_KV), heads(SWA_KV)], out_shape=out_shape,
        compiler_params=_cparams(("parallel", "parallel")), name="odd_prep")(*ins)


SWA_QB = 128


def _swa_kernel(sink_ref, q_ref, kp_ref, kc_ref, kn_ref, vp_ref, vc_ref, vn_ref, o_ref):
    i = pl.program_id(1)
    nb = pl.num_programs(1)
    G = SWA_HEADS // SWA_KV
    span = 3 * SWA_QB
    r = lax.broadcasted_iota(jnp.int32, (SWA_QB, span), 0)
    c = lax.broadcasted_iota(jnp.int32, (SWA_QB, span), 1)
    d = c - r
    kpos_blk = c // SWA_QB + i - 1
    valid = (d >= 0) & (d <= 2 * SWA_WINDOW) & (kpos_blk >= 0) & (kpos_blk < nb)
    bias = jnp.where(valid, 0.0, -jnp.inf).astype(F32)
    lane = lax.broadcasted_iota(jnp.int32, (SWA_QB, LANES), 1)
    low = lane < SWA_D
    for kh in range(SWA_KV):
        k = jnp.concatenate([kp_ref[kh], kc_ref[kh], kn_ref[kh]], axis=0)
        v = jnp.concatenate([vp_ref[kh], vc_ref[kh], vn_ref[kh]], axis=0)
        q = q_ref[kh * G:(kh + 1) * G].reshape(G * SWA_QB, LANES)
        s = lax.dot_general(q, k, (((1,), (1,)), ((), ())), preferred_element_type=F32)
        s = s.reshape(G, SWA_QB, span) + bias[None]
        outs = []
        for g in range(G):
            sg = s[g]
            snk = sink_ref[kh * G + g] * LOG2E
            m = jnp.maximum(jnp.max(sg, axis=-1, keepdims=True), snk)
            p = jnp.exp2(sg - m)
            denom = jnp.sum(p, axis=-1, keepdims=True) + jnp.exp2(snk - m)
            outs.append(_dot(p.astype(BF16), v) / denom)
        for pair in range(G // 2):
            col = kh * (G // 2) + pair
            o_ref[:, col * LANES:(col + 1) * LANES] = jnp.where(low, outs[2 * pair], outs[2 * pair + 1]).astype(o_ref.dtype)


def _swa(sink, q, k, v):
    B, _, S, _ = q.shape
    nb = S // SWA_QB
    kspec = lambda f: pl.BlockSpec((None, SWA_KV, SWA_QB, LANES), lambda b, i: (b, 0, f(i), 0))
    prev = lambda i: jnp.maximum(i - 1, 0)
    cur = lambda i: i
    nxt = lambda i: jnp.minimum(i + 1, nb - 1)
    return pl.pallas_call(
        _swa_kernel, grid=(B, nb),
        in_specs=[pl.BlockSpec(memory_space=pltpu.SMEM),
                  pl.BlockSpec((None, SWA_HEADS, SWA_QB, LANES), lambda b, i: (b, 0, i, 0)),
                  kspec(prev), kspec(cur), kspec(nxt), kspec(prev), kspec(cur), kspec(nxt)],
        out_specs=pl.BlockSpec((None, SWA_QB, SWA_HEADS * SWA_D), lambda b, i: (b, i, 0)),
        out_shape=jax.ShapeDtypeStruct((B, S, SWA_HEADS * SWA_D), BF16),
        compiler_params=_cparams(("parallel", "parallel")), name="swa")(sink, q, k, k, k, v, v, v)


def _rope_cs(pos, dim, theta):
    inv = jnp.float32(theta) ** (-jnp.arange(0, dim, 2, dtype=F32) / dim)
    ang = pos.astype(F32)[:, None] * inv[None, :]
    return jnp.cos(ang), jnp.sin(ang)


def _row(v):
    return v.reshape(1, -1).astype(F32)


def _pad_lanes(v, n=LANES):
    return jnp.pad(v, [(0, 0)] * (v.ndim - 1) + [(0, n - v.shape[-1])])


@jax.jit
def kernel(x, even_norm, even_w_in, mla_q_lat_norm, mla_kv_lat_norm, mla_w_uq, mla_w_ukv, mla_q_norm, mla_k_nope_norm, mla_k_rope_norm, gqa_q_norm, gqa_k_norm, even_w_out, odd_norm, odd_w_qkv, swa_q_norm, swa_k_norm, swa_sink, odd_w_out, mlp_norm, mlp_w_up, mlp_w_down):
    B, S, D = x.shape
    T = B * S
    pos = jnp.arange(S)

    mcos, msin = _rope_cs(pos, MLA_ROPE, ROPE_THETA)
    mla_c = _pad_lanes(jnp.concatenate([mcos, mcos], -1))
    mla_s = _pad_lanes(jnp.concatenate([-msin, msin], -1))
    rcos, rsin = _rope_cs(pos // GRID_W, GQA_D // 2, AXIAL_THETA)
    ccos, csin = _rope_cs(pos % GRID_W, GQA_D // 2, AXIAL_THETA)
    ax_c = jnp.concatenate([rcos, rcos, ccos, ccos], -1)
    ax_s = jnp.concatenate([-rsin, rsin, -csin, csin], -1)
    scos, ssin = _rope_cs(pos, SWA_ROT, ROPE_THETA)
    rest = SWA_D - SWA_ROT
    sw_c1 = jnp.concatenate([scos, scos, jnp.ones((S, rest), F32)], -1)
    sw_s1 = jnp.concatenate([-ssin, ssin, jnp.zeros((S, rest), F32)], -1)
    sw_c = jnp.concatenate([sw_c1, sw_c1], -1)
    sw_s = jnp.concatenate([sw_s1, sw_s1], -1)

    w_in = even_w_in[0]
    o3 = MLA_Q_LORA + MLA_KV_LORA + MLA_ROPE
    w_in = jnp.concatenate([w_in[:, :o3], jnp.zeros((D, LANES - MLA_ROPE), F32), w_in[:, o3:]], -1).astype(BF16)
    wuq = mla_w_uq[0].reshape(MLA_Q_LORA, MLA_HEADS, MLA_QK)
    wuq = jnp.concatenate([wuq[..., :MLA_NOPE].reshape(MLA_Q_LORA, -1),
                           _pad_lanes(wuq[..., MLA_NOPE:]).reshape(MLA_Q_LORA, -1)], -1).astype(BF16)
    wukv = mla_w_ukv[0].astype(BF16)
    qn = mla_q_norm[0]
    mq, mk, mv, gq, gk, gv = _even_prep(
        x, _row(even_norm[0]), w_in, _row(mla_q_lat_norm[0]), _row(mla_kv_lat_norm[0]), wuq, wukv,
        _row(qn[:MLA_NOPE]), _row(_pad_lanes(qn[MLA_NOPE:])), _row(mla_k_nope_norm[0]),
        _row(_pad_lanes(mla_k_rope_norm[0])), _row(gqa_q_norm[0]), _row(gqa_k_norm[0]),
        mla_c, mla_s, ax_c, ax_s)
    o_a = _flash(mq.reshape(B, MLA_HEADS, 1, S, 2 * LANES), mk, mv, tq=1024)
    o_g = _flash(gq.reshape(B, GQA_KV, GQA_HEADS // GQA_KV, S, GQA_D), gk, gv, tq=256)
    w_out = even_w_out[0].astype(BF16)
    na = MLA_HEADS * MLA_V
    xt = _proj_res(x.reshape(T, D), [(o_a.reshape(T, na), w_out[:na]), (o_g.reshape(T, -1), w_out[na:])])
    xt = _mlp(xt, _row(mlp_norm[0]), mlp_w_up[0].astype(BF16), mlp_w_down[0].astype(BF16))

    nq = SWA_HEADS * SWA_D
    nk = SWA_KV * SWA_D
    wq = odd_w_qkv[0]
    dup = lambda w: jnp.concatenate([w.reshape(D, SWA_KV, SWA_D)] * 2, -1).reshape(D, SWA_KV * LANES)
    w_qkv = jnp.concatenate([wq[:, :nq], dup(wq[:, nq:nq + nk]), dup(wq[:, nq + nk:])], -1).astype(BF16)
    lane = jnp.arange(LANES)
    ones_bd = (lane[:, None] // SWA_D == lane[None, :] // SWA_D).astype(BF16)
    two = lambda g: _row(jnp.concatenate([g, g]))
    sq, sk, sv = _odd_prep(xt.reshape(B, S, D), _row(odd_norm[0]), w_qkv, ones_bd,
                           two(swa_q_norm[0]), two(swa_k_norm[0]), sw_c, sw_s)
    o_s = _swa(swa_sink[0].astype(F32), sq, sk, sv)
    xt = _proj_res(xt, [(o_s.reshape(T, nq), odd_w_out[0].astype(BF16))])
    xt = _mlp(xt, _row(mlp_norm[1]), mlp_w_up[1].astype(BF16), mlp_w_down[1].astype(BF16))
    return xt.reshape(B, S, D)
```

```python
import functools
import math

import jax
import jax.numpy as jnp
from jax import lax
from jax.experimental import pallas as pl
from jax.experimental.pallas import tpu as pltpu

D_MODEL = 2048
GRID_W = 64
NORM_EPS = 1e-6
ROPE_THETA = 500000.0
AXIAL_THETA = 10000.0
MLA_HEADS = 8
MLA_Q_LORA = 512
MLA_KV_LORA = 256
MLA_NOPE = 128
MLA_ROPE = 64
MLA_QK = MLA_NOPE + MLA_ROPE
MLA_V = 128
GQA_HEADS = 8
GQA_KV = 2
GQA_D = 128
SWA_HEADS = 32
SWA_KV = 4
SWA_D = 64
SWA_WINDOW = 128
SWA_ROT = SWA_D // 4
D_FF = 4 * D_MODEL

LANES = 128
LOG2E = math.log2(math.e)
VMEM_LIMIT = 56 * 1024 * 1024

F32 = jnp.float32
BF16 = jnp.bfloat16


def _cparams(sem):
    return pltpu.CompilerParams(dimension_semantics=sem, vmem_limit_bytes=VMEM_LIMIT)


def _dot(a, b):
    return jnp.dot(a, b, preferred_element_type=F32)


def _rms(t, gain, n):
    ss = jnp.sum(t * t, axis=-1, keepdims=True)
    return t * lax.rsqrt(ss * (1.0 / n) + NORM_EPS) * gain


def _rot_partner(t, half, group):
    lane = lax.broadcasted_iota(jnp.int32, t.shape, 1)
    first = (lane & (group - 1)) < half
    return jnp.where(first, pltpu.roll(t, LANES - half, 1), pltpu.roll(t, half, 1))


def _rope(t, cos, sin_signed, half, group):
    return t * cos + _rot_partner(t, half, group) * sin_signed


EVEN_W = MLA_Q_LORA + MLA_KV_LORA + LANES + (GQA_HEADS + 2 * GQA_KV) * GQA_D
O_CKV = MLA_Q_LORA
O_KR = O_CKV + MLA_KV_LORA
O_QG = O_KR + LANES
O_KG = O_QG + GQA_HEADS * GQA_D
O_VG = O_KG + GQA_KV * GQA_D


def _even_prep_kernel(x_ref, nrm_ref, win_ref, qlat_ref, kvlat_ref, wuq_ref, wukv_ref,
                      qn_nope_ref, qn_rope_ref, kn_nope_ref, kn_rope_ref, gqn_ref, gkn_ref,
                      mc_ref, ms_ref, ac_ref, as_ref,
                      mq_ref, mk_ref, mv_ref, gq_ref, gk_ref, gv_ref):
    x = x_ref[...]
    h = _rms(x, nrm_ref[...], D_MODEL).astype(BF16)
    proj = _dot(h, win_ref[...])

    mc, ms = mc_ref[...], ms_ref[...]
    ac, asn = ac_ref[...], as_ref[...]
    mla_scale = MLA_QK ** -0.5 * LOG2E
    gqa_scale = GQA_D ** -0.5 * LOG2E

    cq = _rms(proj[:, :O_CKV], qlat_ref[...], MLA_Q_LORA).astype(BF16)
    qa = _dot(cq, wuq_ref[...])
    for hd in range(MLA_HEADS):
        nope = qa[:, hd * LANES:(hd + 1) * LANES]
        rp = qa[:, (MLA_HEADS + hd) * LANES:(MLA_HEADS + hd + 1) * LANES]
        ss = jnp.sum(nope * nope, axis=-1, keepdims=True) + jnp.sum(rp * rp, axis=-1, keepdims=True)
        inv = lax.rsqrt(ss * (1.0 / MLA_QK) + NORM_EPS)
        nope_n = nope * inv * qn_nope_ref[...]
        rp_n = _rope(rp * inv * qn_rope_ref[...], mc, ms, MLA_ROPE // 2, MLA_ROPE)
        mq_ref[hd, :LANES, :] = (nope_n * mla_scale).T.astype(BF16)
        mq_ref[hd, LANES:, :] = (rp_n * mla_scale).T.astype(BF16)

    ckv = _rms(proj[:, O_CKV:O_KR], kvlat_ref[...], MLA_KV_LORA).astype(BF16)
    kv = _dot(ckv, wukv_ref[...])
    k_r = _rope(_rms(proj[:, O_KR:O_QG], kn_rope_ref[...], MLA_ROPE), mc, ms, MLA_ROPE // 2, MLA_ROPE).astype(BF16)
    for hd in range(MLA_HEADS):
        kn = kv[:, 2 * hd * LANES:(2 * hd + 1) * LANES]
        mk_ref[hd, :, :LANES] = _rms(kn, kn_nope_ref[...], MLA_NOPE).astype(BF16)
        mk_ref[hd, :, LANES:] = k_r
        mv_ref[hd] = kv[:, (2 * hd + 1) * LANES:(2 * hd + 2) * LANES].T.astype(BF16)

    for hd in range(GQA_HEADS):
        t = _rms(proj[:, O_QG + hd * GQA_D:O_QG + (hd + 1) * GQA_D], gqn_ref[...], GQA_D)
        gq_ref[hd] = (_rope(t, ac, asn, GQA_D // 4, GQA_D // 2) * gqa_scale).T.astype(BF16)
    for hd in range(GQA_KV):
        t = _rms(proj[:, O_KG + hd * GQA_D:O_KG + (hd + 1) * GQA_D], gkn_ref[...], GQA_D)
        gk_ref[hd] = _rope(t, ac, asn, GQA_D // 4, GQA_D // 2).astype(BF16)
        gv_ref[hd] = proj[:, O_VG + hd * GQA_D:O_VG + (hd + 1) * GQA_D].T.astype(BF16)


def _full(shape):
    nd = len(shape)
    return pl.BlockSpec(shape, lambda *_: (0,) * nd)


def _even_prep(x, nrm, win, qlat, kvlat, wuq, wukv, qn_nope, qn_rope, kn_nope, kn_rope, gqn, gkn,
               mc, ms, ac, asn, tm=256):
    B, S, D = x.shape
    nt = S // tm
    row = lambda w: pl.BlockSpec((None, tm, w), lambda b, i: (b, i, 0))
    tab = pl.BlockSpec((tm, LANES), lambda b, i: (i, 0))
    heads = lambda nh, w: pl.BlockSpec((None, nh, tm, w), lambda b, i: (b, 0, i, 0))
    ins = [x, nrm, win, qlat, kvlat, wuq, wukv, qn_nope, qn_rope, kn_nope, kn_rope, gqn, gkn, mc, ms, ac, asn]
    in_specs = [row(D)] + [_full(a.shape) for a in ins[1:13]] + [tab] * 4
    heads_t = lambda nh, w: pl.BlockSpec((None, nh, w, tm), lambda b, i: (b, 0, 0, i))
    chunks_t = lambda nh, w: pl.BlockSpec((None, nh, None, w, tm), lambda b, i: (b, 0, i, 0, 0))
    out_shape = [
        jax.ShapeDtypeStruct((B, MLA_HEADS, 2 * LANES, S), BF16),
        jax.ShapeDtypeStruct((B, MLA_HEADS, S, 2 * LANES), BF16),
        jax.ShapeDtypeStruct((B, MLA_HEADS, nt, MLA_V, tm), BF16),
        jax.ShapeDtypeStruct((B, GQA_HEADS, GQA_D, S), BF16),
        jax.ShapeDtypeStruct((B, GQA_KV, S, GQA_D), BF16),
        jax.ShapeDtypeStruct((B, GQA_KV, nt, GQA_D, tm), BF16),
    ]
    out_specs = [heads_t(MLA_HEADS, 2 * LANES), heads(MLA_HEADS, 2 * LANES), chunks_t(MLA_HEADS, MLA_V),
                 heads_t(GQA_HEADS, GQA_D), heads(GQA_KV, GQA_D), chunks_t(GQA_KV, GQA_D)]
    return pl.pallas_call(
        _even_prep_kernel, grid=(B, nt), in_specs=in_specs, out_specs=out_specs, out_shape=out_shape,
        compiler_params=_cparams(("parallel", "parallel")), name="even_prep")(*ins)


def _flash_kernel(qt_ref, k_ref, vt_ref, o_ref, m_scr, l_scr, acc_scr, *, tk, n_sub):
    dk, tq = qt_ref.shape
    S = k_ref.shape[0]
    _, dv, cw = vt_ref.shape
    nc = tk // cw
    sub = tq // n_sub
    m_scr[...] = jnp.full(m_scr.shape, -jnp.inf, F32)
    l_scr[...] = jnp.zeros(l_scr.shape, F32)
    acc_scr[...] = jnp.zeros(acc_scr.shape, F32)

    def body(j, carry):
        k = k_ref[pl.ds(pl.multiple_of(j * tk, tk), tk), :]
        vts = [vt_ref[j * nc + c] for c in range(nc)]
        for u in range(n_sub):
            cols = pl.ds(u * sub, sub)
            st = _dot(k, qt_ref[:, cols])
            m_prev = m_scr[:, cols]
            m_new = jnp.maximum(m_prev, jnp.max(st, axis=0, keepdims=True))
            alpha = jnp.exp2(m_prev - m_new)
            p = jnp.exp2(st - m_new)
            l_scr[:, cols] = alpha * l_scr[:, cols] + jnp.sum(p, axis=0, keepdims=True)
            pb = p.astype(BF16)
            pv = _dot(vts[0], pb[:cw])
            for c in range(1, nc):
                pv = pv + _dot(vts[c], pb[c * cw:(c + 1) * cw])
            acc_scr[:, cols] = alpha * acc_scr[:, cols] + pv
            m_scr[:, cols] = m_new
        return carry

    lax.fori_loop(0, S // tk, body, 0)
    o_ref[...] = (acc_scr[...] / l_scr[...]).T.astype(o_ref.dtype)


def _flash(qt, k, vt, G, tq=1024, tk=512, n_sub=1):
    B, H, dk, S = qt.shape
    _, _, nch, dv, cw = vt.shape
    return pl.pallas_call(
        functools.partial(_flash_kernel, tk=tk, n_sub=n_sub),
        grid=(B, H, S // tq),
        in_specs=[pl.BlockSpec((None, None, dk, tq), lambda b, h, i: (b, h, 0, i)),
                  pl.BlockSpec((None, None, S, dk), lambda b, h, i: (b, h // G, 0, 0)),
                  pl.BlockSpec((None, None, nch, dv, cw), lambda b, h, i: (b, h // G, 0, 0, 0))],
        out_specs=pl.BlockSpec((None, tq, dv), lambda b, h, i: (b, i, h)),
        out_shape=jax.ShapeDtypeStruct((B, S, H * dv), BF16),
        scratch_shapes=[pltpu.VMEM((1, tq), F32), pltpu.VMEM((1, tq), F32), pltpu.VMEM((dv, tq), F32)],
        compiler_params=_cparams(("parallel", "parallel", "arbitrary")), name="flash")(qt, k, vt)


def _proj_res_kernel(*refs):
    x_ref, o_ref = refs[0], refs[-1]
    acc = x_ref[...]
    for a_ref, w_ref in zip(refs[1:-1:2], refs[2:-1:2]):
        acc = acc + _dot(a_ref[...], w_ref[...])
    o_ref[...] = acc


def _proj_res(x, pairs, tm=512):
    T, D = x.shape
    ins, in_specs = [x], [pl.BlockSpec((tm, D), lambda i: (i, 0))]
    for a, w in pairs:
        ins += [a, w]
        in_specs += [pl.BlockSpec((tm, a.shape[1]), lambda i: (i, 0)), _full(w.shape)]
    return pl.pallas_call(
        _proj_res_kernel, grid=(T // tm,), in_specs=in_specs,
        out_specs=pl.BlockSpec((tm, D), lambda i: (i, 0)),
        out_shape=jax.ShapeDtypeStruct((T, D), F32),
        compiler_params=_cparams(("parallel",)), name="proj_res")(*ins)


def _mlp_kernel(x_ref, g_ref, wu_ref, wd_ref, o_ref, xn_scr):
    j = pl.program_id(1)

    @pl.when(j == 0)
    def _():
        x = x_ref[...]
        xn_scr[...] = _rms(x, g_ref[...], D_MODEL).astype(BF16)
        o_ref[...] = x

    h = _dot(xn_scr[...], wu_ref[...])
    h = jnp.square(jnp.maximum(h, 0.0)).astype(BF16)
    o_ref[...] += _dot(h, wd_ref[...])


def _mlp(x, gain, wu, wd, tm=512, fc=1024):
    T, D = x.shape
    F = wu.shape[1]
    return pl.pallas_call(
        _mlp_kernel, grid=(T // tm, F // fc),
        in_specs=[pl.BlockSpec((tm, D), lambda i, j: (i, 0)), _full(gain.shape),
                  pl.BlockSpec((D, fc), lambda i, j: (0, j)), pl.BlockSpec((fc, D), lambda i, j: (j, 0))],
        out_specs=pl.BlockSpec((tm, D), lambda i, j: (i, 0)),
        out_shape=jax.ShapeDtypeStruct((T, D), F32),
        scratch_shapes=[pltpu.VMEM((tm, D), BF16)],
        compiler_params=_cparams(("parallel", "arbitrary")), name="mlp")(x, gain, wu, wd)


ODD_W = SWA_HEADS * SWA_D + 2 * SWA_KV * LANES


def _odd_prep_kernel(x_ref, nrm_ref, w_ref, ones_ref, qn_ref, kn_ref, c_ref, s_ref, q_ref, k_ref, v_ref):
    x = x_ref[...]
    h = _rms(x, nrm_ref[...], D_MODEL).astype(BF16)
    qkv = _dot(h, w_ref[...])
    c, sn = c_ref[...], s_ref[...]
    scale = SWA_D ** -0.5 * LOG2E
    lane = lax.broadcasted_iota(jnp.int32, (x.shape[0], LANES), 1)
    low = lane < SWA_D
    nq = SWA_HEADS * SWA_D

    def norm_rope(t, gain):
        sq = t * t
        hi = sq.astype(BF16)
        lo = (sq - hi.astype(F32)).astype(BF16)
        ss = _dot(hi, ones_ref[...]) + _dot(lo, ones_ref[...])
        tn = t * lax.rsqrt(ss * (1.0 / SWA_D) + NORM_EPS) * gain
        return _rope(tn, c, sn, SWA_ROT // 2, SWA_D)

    for cidx in range(nq // LANES):
        t = norm_rope(qkv[:, cidx * LANES:(cidx + 1) * LANES], qn_ref[...]) * scale
        q_ref[2 * cidx] = jnp.where(low, t, 0.0).astype(BF16)
        q_ref[2 * cidx + 1] = jnp.where(low, 0.0, t).astype(BF16)
    for hd in range(SWA_KV):
        k_ref[hd] = norm_rope(qkv[:, nq + hd * LANES:nq + (hd + 1) * LANES], kn_ref[...]).astype(BF16)
        v_ref[hd] = qkv[:, nq + (SWA_KV + hd) * LANES:nq + (SWA_KV + hd + 1) * LANES].astype(BF16)


def _odd_prep(x, nrm, w, ones_bd, qn, kn, c, sn, tm=256):
    B, S, D = x.shape
    heads = lambda nh: pl.BlockSpec((None, nh, tm, LANES), lambda b, i: (b, 0, i, 0))
    ins = [x, nrm, w, ones_bd, qn, kn, c, sn]
    in_specs = ([pl.BlockSpec((None, tm, D), lambda b, i: (b, i, 0))] + [_full(a.shape) for a in ins[1:6]]
                + [pl.BlockSpec((tm, LANES), lambda b, i: (i, 0))] * 2)
    out_shape = [jax.ShapeDtypeStruct((B, SWA_HEADS, S, LANES), BF16),
                 jax.ShapeDtypeStruct((B, SWA_KV, S, LANES), BF16),
                 jax.ShapeDtypeStruct((B, SWA_KV, S, LANES), BF16)]
    return pl.pallas_call(
        _odd_prep_kernel, grid=(B, S // tm), in_specs=in_specs,
        out_specs=[heads(SWA_HEADS), heads(SWA_KV), heads(SWA_KV)], out_shape=out_shape,
        compiler_params=_cparams(("parallel", "parallel")), name="odd_prep")(*ins)


SWA_QB = 128


def _swa_kernel(sink_ref, q_ref, kp_ref, kc_ref, kn_ref, vp_ref, vc_ref, vn_ref, o_ref):
    i = pl.program_id(1)
    nb = pl.num_programs(1)
    G = SWA_HEADS // SWA_KV
    span = 3 * SWA_QB
    r = lax.broadcasted_iota(jnp.int32, (SWA_QB, span), 0)
    c = lax.broadcasted_iota(jnp.int32, (SWA_QB, span), 1)
    d = c - r
    kpos_blk = c // SWA_QB + i - 1
    valid = (d >= 0) & (d <= 2 * SWA_WINDOW) & (kpos_blk >= 0) & (kpos_blk < nb)
    bias = jnp.where(valid, 0.0, -jnp.inf).astype(F32)
    lane = lax.broadcasted_iota(jnp.int32, (SWA_QB, LANES), 1)
    low = lane < SWA_D
    for kh in range(SWA_KV):
        k = jnp.concatenate([kp_ref[kh], kc_ref[kh], kn_ref[kh]], axis=0)
        v = jnp.concatenate([vp_ref[kh], vc_ref[kh], vn_ref[kh]], axis=0)
        q = q_ref[kh * G:(kh + 1) * G].reshape(G * SWA_QB, LANES)
        s = lax.dot_general(q, k, (((1,), (1,)), ((), ())), preferred_element_type=F32)
        s = s.reshape(G, SWA_QB, span) + bias[None]
        outs = []
        for g in range(G):
            sg = s[g]
            snk = sink_ref[kh * G + g] * LOG2E
            m = jnp.maximum(jnp.max(sg, axis=-1, keepdims=True), snk)
            p = jnp.exp2(sg - m)
            denom = jnp.sum(p, axis=-1, keepdims=True) + jnp.exp2(snk - m)
            outs.append(_dot(p.astype(BF16), v) / denom)
        for pair in range(G // 2):
            col = kh * (G // 2) + pair
            o_ref[:, col * LANES:(col + 1) * LANES] = jnp.where(low, outs[2 * pair], outs[2 * pair + 1]).astype(o_ref.dtype)


def _swa(sink, q, k, v):
    B, _, S, _ = q.shape
    nb = S // SWA_QB
    kspec = lambda f: pl.BlockSpec((None, SWA_KV, SWA_QB, LANES), lambda b, i: (b, 0, f(i), 0))
    prev = lambda i: jnp.maximum(i - 1, 0)
    cur = lambda i: i
    nxt = lambda i: jnp.minimum(i + 1, nb - 1)
    return pl.pallas_call(
        _swa_kernel, grid=(B, nb),
        in_specs=[pl.BlockSpec(memory_space=pltpu.SMEM),
                  pl.BlockSpec((None, SWA_HEADS, SWA_QB, LANES), lambda b, i: (b, 0, i, 0)),
                  kspec(prev), kspec(cur), kspec(nxt), kspec(prev), kspec(cur), kspec(nxt)],
        out_specs=pl.BlockSpec((None, SWA_QB, SWA_HEADS * SWA_D), lambda b, i: (b, i, 0)),
        out_shape=jax.ShapeDtypeStruct((B, S, SWA_HEADS * SWA_D), BF16),
        compiler_params=_cparams(("parallel", "parallel")), name="swa")(sink, q, k, k, k, v, v, v)


def _rope_cs(pos, dim, theta):
    inv = jnp.float32(theta) ** (-jnp.arange(0, dim, 2, dtype=F32) / dim)
    ang = pos.astype(F32)[:, None] * inv[None, :]
    return jnp.cos(ang), jnp.sin(ang)


def _row(v):
    return v.reshape(1, -1).astype(F32)


def _pad_lanes(v, n=LANES):
    return jnp.pad(v, [(0, 0)] * (v.ndim - 1) + [(0, n - v.shape[-1])])


@jax.jit
def kernel(x, even_norm, even_w_in, mla_q_lat_norm, mla_kv_lat_norm, mla_w_uq, mla_w_ukv, mla_q_norm, mla_k_nope_norm, mla_k_rope_norm, gqa_q_norm, gqa_k_norm, even_w_out, odd_norm, odd_w_qkv, swa_q_norm, swa_k_norm, swa_sink, odd_w_out, mlp_norm, mlp_w_up, mlp_w_down):
    B, S, D = x.shape
    T = B * S
    pos = jnp.arange(S)

    mcos, msin = _rope_cs(pos, MLA_ROPE, ROPE_THETA)
    mla_c = _pad_lanes(jnp.concatenate([mcos, mcos], -1))
    mla_s = _pad_lanes(jnp.concatenate([-msin, msin], -1))
    rcos, rsin = _rope_cs(pos // GRID_W, GQA_D // 2, AXIAL_THETA)
    ccos, csin = _rope_cs(pos % GRID_W, GQA_D // 2, AXIAL_THETA)
    ax_c = jnp.concatenate([rcos, rcos, ccos, ccos], -1)
    ax_s = jnp.concatenate([-rsin, rsin, -csin, csin], -1)
    scos, ssin = _rope_cs(pos, SWA_ROT, ROPE_THETA)
    rest = SWA_D - SWA_ROT
    sw_c1 = jnp.concatenate([scos, scos, jnp.ones((S, rest), F32)], -1)
    sw_s1 = jnp.concatenate([-ssin, ssin, jnp.zeros((S, rest), F32)], -1)
    sw_c = jnp.concatenate([sw_c1, sw_c1], -1)
    sw_s = jnp.concatenate([sw_s1, sw_s1], -1)

    w_in = even_w_in[0]
    o3 = MLA_Q_LORA + MLA_KV_LORA + MLA_ROPE
    w_in = jnp.concatenate([w_in[:, :o3], jnp.zeros((D, LANES - MLA_ROPE), F32), w_in[:, o3:]], -1).astype(BF16)
    wuq = mla_w_uq[0].reshape(MLA_Q_LORA, MLA_HEADS, MLA_QK)
    wuq = jnp.concatenate([wuq[..., :MLA_NOPE].reshape(MLA_Q_LORA, -1),
                           _pad_lanes(wuq[..., MLA_NOPE:]).reshape(MLA_Q_LORA, -1)], -1).astype(BF16)
    wukv = mla_w_ukv[0].astype(BF16)
    qn = mla_q_norm[0]
    mq, mk, mv, gq, gk, gv = _even_prep(
        x, _row(even_norm[0]), w_in, _row(mla_q_lat_norm[0]), _row(mla_kv_lat_norm[0]), wuq, wukv,
        _row(qn[:MLA_NOPE]), _row(_pad_lanes(qn[MLA_NOPE:])), _row(mla_k_nope_norm[0]),
        _row(_pad_lanes(mla_k_rope_norm[0])), _row(gqa_q_norm[0]), _row(gqa_k_norm[0]),
        mla_c, mla_s, ax_c, ax_s)
    o_a = _flash(mq, mk, mv, G=1)
    o_g = _flash(gq, gk, gv, G=GQA_HEADS // GQA_KV)
    w_out = even_w_out[0].astype(BF16)
    na = MLA_HEADS * MLA_V
    xt = _proj_res(x.reshape(T, D), [(o_a.reshape(T, na), w_out[:na]), (o_g.reshape(T, -1), w_out[na:])])
    xt = _mlp(xt, _row(mlp_norm[0]), mlp_w_up[0].astype(BF16), mlp_w_down[0].astype(BF16))

    nq = SWA_HEADS * SWA_D
    nk = SWA_KV * SWA_D
    wq = odd_w_qkv[0]
    dup = lambda w: jnp.concatenate([w.reshape(D, SWA_KV, SWA_D)] * 2, -1).reshape(D, SWA_KV * LANES)
    w_qkv = jnp.concatenate([wq[:, :nq], dup(wq[:, nq:nq + nk]), dup(wq[:, nq + nk:])], -1).astype(BF16)
    lane = jnp.arange(LANES)
    ones_bd = (lane[:, None] // SWA_D == lane[None, :] // SWA_D).astype(BF16)
    two = lambda g: _row(jnp.concatenate([g, g]))
    sq, sk, sv = _odd_prep(xt.reshape(B, S, D), _row(odd_norm[0]), w_qkv, ones_bd,
                           two(swa_q_norm[0]), two(swa_k_norm[0]), sw_c, sw_s)
    o_s = _swa(swa_sink[0].astype(F32), sq, sk, sv)
    xt = _proj_res(xt, [(o_s.reshape(T, nq), odd_w_out[0].astype(BF16))])
    xt = _mlp(xt, _row(mlp_norm[1]), mlp_w_up[1].astype(BF16), mlp_w_down[1].astype(BF16))
    return xt.reshape(B, S, D)
```
